```python
import math
import jax
import jax.numpy as jnp
from jax import lax
import numpy as np

D_MODEL = 2048
BATCH = 2
SEQ = 4096
DEPTH = 4

D_MIX = D_MODEL
RMS_EPS = 1e-5
SSD_WIDTH = D_MIX // 2
SSD_HEAD_DIM = 64
SSD_HEADS = SSD_WIDTH // SSD_HEAD_DIM
SSD_GROUPS = 2
SSD_STATE = 128
SSD_CONV = 4
SSD_CHUNK = 128
SSD_CONV_CH = SSD_WIDTH + 2 * SSD_GROUPS * SSD_STATE
SSD_SIZES = (SSD_WIDTH, SSD_CONV_CH, SSD_HEADS)
RWKV_WIDTH = D_MIX // 4
RWKV_HEAD_DIM = 64
RWKV_HEADS = RWKV_WIDTH // RWKV_HEAD_DIM
RWKV_DECAY_LORA = 64
RWKV_AAA_LORA = 64
RWKV_GATE_LORA = 128
RWKV_LN_EPS = 64e-5
RWKV_SIZES = (RWKV_WIDTH, RWKV_DECAY_LORA, RWKV_WIDTH, RWKV_WIDTH, RWKV_AAA_LORA, RWKV_GATE_LORA)
RET_WIDTH = D_MIX - SSD_WIDTH - RWKV_WIDTH
RET_HEADS = 4
RET_V_DIM = RET_WIDTH // RET_HEADS
RET_QK_DIM = RET_V_DIM // 2
RET_CHUNK = 128
ROPE_BASE = 10000.0
RET_SIZES = (RET_HEADS * RET_QK_DIM, RET_HEADS * RET_QK_DIM, RET_WIDTH, RET_WIDTH)
SSD_COLS = sum(SSD_SIZES)
RWKV_COLS = sum(RWKV_SIZES)
RET_COLS = sum(RET_SIZES)
N_IN = SSD_COLS + RWKV_COLS + RET_COLS
N_EXPERTS = 32
TOP_K = 4
D_EXPERT = D_MODEL // 4
SWIGLU_ALPHA = 1.702
SWIGLU_LIMIT = 7.0

kernel_name = 'hybrid_ssd_rwkv7_retention_moe_adaln'


def rms_norm(x, w, eps=RMS_EPS):
    xf = x.astype(jnp.float32)
    y = xf * lax.rsqrt(jnp.mean(xf * xf, axis=-1, keepdims=True) + eps)
    return (y * w.astype(jnp.float32)).astype(x.dtype)


def split_cols(a, sizes):
    cuts, acc = [], 0
    for s in sizes[:-1]:
        acc += s
        cuts.append(acc)
    return jnp.split(a, cuts, axis=-1)


def rotary(t, positions):
    half = t.shape[-1] // 2
    inv_freq = ROPE_BASE ** (-jnp.arange(half, dtype=jnp.float32) / half)
    ang = positions.astype(jnp.float32)[..., None] * inv_freq
    cos = jnp.cos(ang)[:, :, None, :]
    sin = jnp.sin(ang)[:, :, None, :]
    t1, t2 = t[..., :half], t[..., half:]
    return jnp.concatenate([t1 * cos - t2 * sin, t1 * sin + t2 * cos], axis=-1)


def causal_depthwise_conv(u, w, b):
    ch, width = w.shape
    rhs = jnp.transpose(w)[:, None, :].astype(u.dtype)
    out = lax.conv_general_dilated(u, rhs, window_strides=(1,), padding=[(width - 1, 0)],
                                   dimension_numbers=('NWC', 'WIO', 'NWC'), feature_group_count=ch)
    return out + b.astype(u.dtype)


def segsum(a):
    n = a.shape[-1]
    a_rep = jnp.broadcast_to(a[..., :, None], a.shape + (n,))
    strict = jnp.tril(jnp.ones((n, n), dtype=bool), -1)
    s = jnp.cumsum(jnp.where(strict, a_rep, 0.0), axis=-2)
    return jnp.where(jnp.tril(jnp.ones((n, n), dtype=bool)), s, -jnp.inf)


def ssd_chunked(xdt, da, bh, ch):
    b, t, h, p = xdt.shape
    n = bh.shape[-1]
    nc, cl = t // SSD_CHUNK, SSD_CHUNK
    xc = xdt.reshape(b, nc, cl, h, p)
    bc = bh.reshape(b, nc, cl, h, n)
    cc = ch.reshape(b, nc, cl, h, n)
    ac = jnp.transpose(da.reshape(b, nc, cl, h), (0, 3, 1, 2))
    a_cum = jnp.cumsum(ac, axis=-1)
    scores = jnp.einsum('bclhn,bcshn->bhcls', cc, bc) * jnp.exp(segsum(ac))
    y_diag = jnp.einsum('bhcls,bcshp->bclhp', scores, xc)
    decay_states = jnp.exp(a_cum[..., -1:] - a_cum)
    states = jnp.einsum('bclhn,bhcl,bclhp->bchpn', bc, decay_states, xc)
    states = jnp.concatenate([jnp.zeros_like(states[:, :1]), states], axis=1)
    chunk_decay = jnp.exp(segsum(jnp.pad(a_cum[..., -1], ((0, 0), (0, 0), (1, 0)))))
    states = jnp.einsum('bhzc,bchpn->bzhpn', chunk_decay, states)[:, :-1]
    y_off = jnp.einsum('bclhn,bchpn,bhcl->bclhp', cc, states, jnp.exp(a_cum))
    return (y_diag + y_off).reshape(b, t, h, p)


def ssd_group(p, conv_w, conv_b, dt_bias, a_log, d_skip, norm_w):
    b, t, _ = p.shape
    f32 = jnp.float32
    z, xbc, dt = split_cols(p, SSD_SIZES)
    xbc = jax.nn.silu(causal_depthwise_conv(xbc, conv_w, conv_b)).astype(f32)
    xs, bm, cm = split_cols(xbc, (SSD_WIDTH, SSD_GROUPS * SSD_STATE, SSD_GROUPS * SSD_STATE))
    heads_per_group = SSD_HEADS // SSD_GROUPS
    xs = xs.reshape(b, t, SSD_HEADS, SSD_HEAD_DIM)
    bm = jnp.repeat(bm.reshape(b, t, SSD_GROUPS, SSD_STATE), heads_per_group, axis=2)
    cm = jnp.repeat(cm.reshape(b, t, SSD_GROUPS, SSD_STATE), heads_per_group, axis=2)
    dt = jax.nn.softplus(dt.astype(f32) + dt_bias.astype(f32))
    a = -jnp.exp(a_log.astype(f32))
    y = ssd_chunked(xs * dt[..., None], dt * a, bm, cm)
    y = y + d_skip.astype(f32)[:, None] * xs
    y = y.reshape(b, t, SSD_WIDTH) * jax.nn.silu(z.astype(f32))
    yg = y.reshape(b, t, SSD_GROUPS, SSD_WIDTH // SSD_GROUPS)
    yg = yg * lax.rsqrt(jnp.mean(yg * yg, axis=-1, keepdims=True) + RMS_EPS)
    return (yg.reshape(b, t, SSD_WIDTH) * norm_w.astype(f32)).astype(p.dtype)


def rwkv7_scan(r, w, k, v, a, bb):
    b, t, h, n = r.shape

    def step(state, inp):
        r_t, w_t, k_t, v_t, a_t, b_t = inp
        sa = jnp.einsum('bhvk,bhk->bhv', state, a_t)
        state = state * w_t[:, :, None, :] + sa[..., None] * b_t[:, :, None, :] + v_t[..., None] * k_t[:, :, None, :]
        return state, jnp.einsum('bhvk,bhk->bhv', state, r_t)

    tm = lambda u: jnp.moveaxis(u, 1, 0)
    s0 = jnp.zeros((b, h, n, n), jnp.float32)
    _, y = lax.scan(step, s0, (tm(r), tm(w), tm(k), tm(v), tm(a), tm(bb)))
    return jnp.moveaxis(y, 0, 1)


def rwkv7_group(p, mu, w0, w_up, a0, a_up, g_up, k_k, k_a, r_k, ln_w, ln_b):
    b, t, _ = p.shape
    f32 = jnp.float32
    pf = p.astype(f32)
    prev = jnp.pad(pf, ((0, 0), (1, 0), (0, 0)))[:, :-1]
    pf = pf + (prev - pf) * mu.astype(f32)
    r, w_lo, k, v, a_lo, g_lo = split_cols(pf, RWKV_SIZES)
    w = -jax.nn.softplus(-(w0.astype(f32) + jnp.tanh(w_lo) @ w_up.astype(f32))) - 0.5
    decay = jnp.exp(-jnp.exp(w))
    a = jax.nn.sigmoid(a0.astype(f32) + a_lo @ a_up.astype(f32))
    g = jax.nn.sigmoid(g_lo) @ g_up.astype(f32)
    heads = lambda u: u.reshape(b, t, RWKV_HEADS, RWKV_HEAD_DIM)
    kk = heads(k * k_k.astype(f32))
    kk = kk / jnp.maximum(jnp.linalg.norm(kk, axis=-1, keepdims=True), 1e-12)
    k = heads(k * (1.0 + (a - 1.0) * k_a.astype(f32)))
    r, v, a, decay = heads(r), heads(v), heads(a), heads(decay)
    y = rwkv7_scan(r, decay, k, v, -kk, kk * a)
    mean = jnp.mean(y, axis=-1, keepdims=True)
    var = jnp.mean(jnp.square(y - mean), axis=-1, keepdims=True)
    y = ((y - mean) * lax.rsqrt(var + RWKV_LN_EPS)).reshape(b, t, RWKV_WIDTH)
    y = y * ln_w.astype(f32) + ln_b.astype(f32)
    bonus = jnp.sum(r * k * r_k.astype(f32), axis=-1, keepdims=True) * v
    y = y + bonus.reshape(b, t, RWKV_WIDTH)
    return (y * g).astype(p.dtype)


def chunkwise_retention(q, k, v, log_gamma):
    b, t, h, dk = q.shape
    dv = v.shape[-1]
    nc, cl = t // RET_CHUNK, RET_CHUNK
    to_chunks = lambda u: jnp.transpose(u.reshape(b, nc, cl, h, u.shape[-1]), (1, 0, 3, 2, 4))
    idx = jnp.arange(cl, dtype=jnp.float32)
    diff = idx[:, None] - idx[None, :]
    causal = diff >= 0
    inner_decay = jnp.where(causal, jnp.exp(jnp.where(causal, diff, 0.0)[None] * log_gamma[:, None, None]), 0.0)
    q_decay = jnp.exp((idx + 1.0)[None, :] * log_gamma[:, None])
    k_decay = jnp.exp((cl - 1.0 - idx)[None, :] * log_gamma[:, None])
    chunk_decay = jnp.exp(cl * log_gamma)

    def step(state, qkv):
        qc, kc, vc = qkv
        scores = jnp.einsum('bhnd,bhmd->bhnm', qc, kc) * inner_decay
        out = jnp.einsum('bhnm,bhme->bhne', scores, vc) + jnp.einsum('bhnd,bhde->bhne', qc, state) * q_decay[..., None]
        state = state * chunk_decay[:, None, None] + jnp.einsum('bhmd,bhme->bhde', kc * k_decay[..., None], vc)
        return state, out

    s0 = jnp.zeros((b, h, dk, dv), jnp.float32)
    _, out = lax.scan(step, s0, (to_chunks(q), to_chunks(k), to_chunks(v)))
    return jnp.transpose(out, (1, 0, 3, 2, 4)).reshape(b, t, h, dv)


def retention_group(p, positions):
    b, t, _ = p.shape
    q, k, v, g = split_cols(p.astype(jnp.float32), RET_SIZES)
    q = rotary(q.reshape(b, t, RET_HEADS, RET_QK_DIM), positions)
    k = rotary(k.reshape(b, t, RET_HEADS, RET_QK_DIM), positions) * (RET_QK_DIM ** -0.5)
    v = v.reshape(b, t, RET_HEADS, RET_V_DIM)
    log_gamma = jnp.log1p(-jnp.exp2(-5.0 - jnp.arange(RET_HEADS, dtype=jnp.float32)))
    y = chunkwise_retention(q, k, v, log_gamma)
    y = y * lax.rsqrt(jnp.mean(y * y, axis=-1, keepdims=True) + RMS_EPS)
    return (jax.nn.silu(g) * y.reshape(b, t, RET_WIDTH)).astype(p.dtype)


def hybrid_mixer(h, positions, w_in, w_out, ssd_conv_w, ssd_conv_b, ssd_dt_bias, ssd_a_log, ssd_d, ssd_norm,
                 rwkv_mu, rwkv_w0, rwkv_w_up, rwkv_a0, rwkv_a_up, rwkv_g_up, rwkv_k_k, rwkv_k_a, rwkv_r_k,
                 rwkv_ln_w, rwkv_ln_b):
    proj = h @ w_in
    p_ssd, p_rwkv, p_ret = split_cols(proj, (SSD_COLS, RWKV_COLS, RET_COLS))
    y_ssd = ssd_group(p_ssd, ssd_conv_w, ssd_conv_b, ssd_dt_bias, ssd_a_log, ssd_d, ssd_norm)
    y_rwkv = rwkv7_group(p_rwkv, rwkv_mu, rwkv_w0, rwkv_w_up, rwkv_a0, rwkv_a_up, rwkv_g_up,
                         rwkv_k_k, rwkv_k_a, rwkv_r_k, rwkv_ln_w, rwkv_ln_b)
    y_ret = retention_group(p_ret, positions)
    y = jnp.concatenate([y_ssd, y_rwkv, y_ret], axis=-1)
    return y @ w_out


def moe_ffn(h, router_w, router_b, w_gate_up, b_gate_up, w_down, b_down):
    b, t, d = h.shape
    tok = h.reshape(b * t, d)
    logits = tok.astype(jnp.float32) @ router_w.astype(jnp.float32) + router_b.astype(jnp.float32)
    top_logits, top_idx = lax.top_k(logits, TOP_K)
    top_w = jax.nn.softmax(top_logits, axis=-1)
    combine = jnp.einsum('nk,nke->ne', top_w, jax.nn.one_hot(top_idx, N_EXPERTS, dtype=jnp.float32))
    out = jnp.zeros((b * t, d), jnp.float32)
    for e in range(N_EXPERTS):
        gu = tok @ w_gate_up[e] + b_gate_up[e]
        gate = jnp.minimum(gu[:, 0::2], SWIGLU_LIMIT)
        up = jnp.clip(gu[:, 1::2], -SWIGLU_LIMIT, SWIGLU_LIMIT)
        act = (up + 1.0) * gate * jax.nn.sigmoid(gate * SWIGLU_ALPHA)
        out = out + combine[:, e:e + 1] * (act @ w_down[e] + b_down[e]).astype(jnp.float32)
    return out.astype(h.dtype).reshape(b, t, d)


def setup_inputs(seed: int = 0) -> dict:
    key = jax.random.key(seed)
    ks = iter(jax.random.split(key, 40))
    f32 = jnp.float32

    def nrm(shape, scale):
        return jax.random.normal(next(ks), shape, f32) * scale

    def gain(shape):
        return 1.0 + nrm(shape, 0.02)

    def unif(shape, lo, hi):
        return jax.random.uniform(next(ks), shape, f32, lo, hi)

    x = nrm((BATCH, SEQ, D_MODEL), 1.0)
    c = nrm((BATCH, D_MODEL), 1.0)
    offset = jax.random.randint(next(ks), (BATCH, 1), 0, 1024, dtype=jnp.int32)
    positions = offset + jnp.arange(SEQ, dtype=jnp.int32)[None, :]
    ada_w = nrm((DEPTH, D_MODEL, 6 * D_MODEL), 0.5 * D_MODEL ** -0.5)
    ada_b = nrm((DEPTH, 6 * D_MODEL), 0.02)
    norm_mix = gain((DEPTH, D_MODEL))
    norm_ffn = gain((DEPTH, D_MODEL))
    norm_final = gain((D_MODEL,))
    w_in = nrm((DEPTH, D_MODEL, N_IN), D_MODEL ** -0.5)
    w_out = nrm((DEPTH, D_MIX, D_MODEL), D_MIX ** -0.5)
    ssd_conv_w = nrm((DEPTH, SSD_CONV_CH, SSD_CONV), SSD_CONV ** -0.5)
    ssd_conv_b = nrm((DEPTH, SSD_CONV_CH), 0.02)
    dt = jnp.exp(unif((DEPTH, SSD_HEADS), math.log(1e-3), math.log(1e-1)))
    ssd_dt_bias = dt + jnp.log(-jnp.expm1(-dt))
    ssd_a_log = jnp.log(unif((DEPTH, SSD_HEADS), 1.0, 16.0))
    ssd_d = gain((DEPTH, SSD_HEADS))
    ssd_norm = gain((DEPTH, SSD_WIDTH))
    rwkv_mu = unif((DEPTH, RWKV_COLS), 0.0, 1.0)
    rwkv_w0 = unif((DEPTH, RWKV_WIDTH), -6.0, -1.0)
    rwkv_w_up = nrm((DEPTH, RWKV_DECAY_LORA, RWKV_WIDTH), 0.1)
    rwkv_a0 = nrm((DEPTH, RWKV_WIDTH), 0.1)
    rwkv_a_up = nrm((DEPTH, RWKV_AAA_LORA, RWKV_WIDTH), 0.1)
    rwkv_g_up = nrm((DEPTH, RWKV_GATE_LORA, RWKV_WIDTH), RWKV_GATE_LORA ** -0.5)
    rwkv_k_k = 0.85 + nrm((DEPTH, RWKV_WIDTH), 0.02)
    rwkv_k_a = gain((DEPTH, RWKV_WIDTH))
    rwkv_r_k = nrm((DEPTH, RWKV_HEADS, RWKV_HEAD_DIM), 0.1)
    rwkv_ln_w = gain((DEPTH, RWKV_WIDTH))
    rwkv_ln_b = nrm((DEPTH, RWKV_WIDTH), 0.02)
    router_w = nrm((DEPTH, D_MODEL, N_EXPERTS), D_MODEL ** -0.5)
    router_b = nrm((DEPTH, N_EXPERTS), 0.01)
    moe_w_gate_up = nrm((DEPTH, N_EXPERTS, D_MODEL, 2 * D_EXPERT), D_MODEL ** -0.5)
    moe_b_gate_up = nrm((DEPTH, N_EXPERTS, 2 * D_EXPERT), 0.01)
    moe_w_down = nrm((DEPTH, N_EXPERTS, D_EXPERT, D_MODEL), D_EXPERT ** -0.5)
    moe_b_down = nrm((DEPTH, N_EXPERTS, D_MODEL), 0.01)
    return {'x': x, 'c': c, 'positions': positions, 'ada_w': ada_w, 'ada_b': ada_b,
            'norm_mix': norm_mix, 'norm_ffn': norm_ffn, 'norm_final': norm_final,
            'w_in': w_in, 'w_out': w_out,
            'ssd_conv_w': ssd_conv_w, 'ssd_conv_b': ssd_conv_b, 'ssd_dt_bias': ssd_dt_bias,
            'ssd_a_log': ssd_a_log, 'ssd_d': ssd_d, 'ssd_norm': ssd_norm,
            'rwkv_mu': rwkv_mu, 'rwkv_w0': rwkv_w0, 'rwkv_w_up': rwkv_w_up, 'rwkv_a0': rwkv_a0,
            'rwkv_a_up': rwkv_a_up, 'rwkv_g_up': rwkv_g_up, 'rwkv_k_k': rwkv_k_k, 'rwkv_k_a': rwkv_k_a,
            'rwkv_r_k': rwkv_r_k, 'rwkv_ln_w': rwkv_ln_w, 'rwkv_ln_b': rwkv_ln_b,
            'router_w': router_w, 'router_b': router_b, 'moe_w_gate_up': moe_w_gate_up,
            'moe_b_gate_up': moe_b_gate_up, 'moe_w_down': moe_w_down, 'moe_b_down': moe_b_down}


def reference(x, c, positions, ada_w, ada_b, norm_mix, norm_ffn, norm_final, w_in, w_out,
              ssd_conv_w, ssd_conv_b, ssd_dt_bias, ssd_a_log, ssd_d, ssd_norm,
              rwkv_mu, rwkv_w0, rwkv_w_up, rwkv_a0, rwkv_a_up, rwkv_g_up, rwkv_k_k, rwkv_k_a,
              rwkv_r_k, rwkv_ln_w, rwkv_ln_b,
              router_w, router_b, moe_w_gate_up, moe_b_gate_up, moe_w_down, moe_b_down):
    cond = jax.nn.silu(c)
    for l in range(DEPTH):
        mod = (cond @ ada_w[l] + ada_b[l])[:, None, :]
        sh_m, sc_m, g_m, sh_f, sc_f, g_f = jnp.split(mod, 6, axis=-1)
        h = rms_norm(x, norm_mix[l]) * (1.0 + sc_m) + sh_m
        x = x + g_m * hybrid_mixer(h, positions, w_in[l], w_out[l],
                                   ssd_conv_w[l], ssd_conv_b[l], ssd_dt_bias[l], ssd_a_log[l], ssd_d[l], ssd_norm[l],
                                   rwkv_mu[l], rwkv_w0[l], rwkv_w_up[l], rwkv_a0[l], rwkv_a_up[l], rwkv_g_up[l],
                                   rwkv_k_k[l], rwkv_k_a[l], rwkv_r_k[l], rwkv_ln_w[l], rwkv_ln_b[l])
        h = rms_norm(x, norm_ffn[l]) * (1.0 + sc_f) + sh_f
        x = x + g_f * moe_ffn(h, router_w[l], router_b[l], moe_w_gate_up[l], moe_b_gate_up[l],
                              moe_w_down[l], moe_b_down[l])
    return rms_norm(x, norm_final)
```

```python
import functools
import math

import jax
import jax.numpy as jnp
from jax import lax
from jax.experimental import pallas as pl
from jax.experimental.pallas import tpu as pltpu

F32 = jnp.float32
BF16 = jnp.bfloat16

RMS_EPS = 1e-5
SSD_WIDTH = 1024
SSD_HEADS = 16
SSD_HEAD_DIM = 64
SSD_GROUPS = 2
SSD_STATE = 128
SSD_CONV = 4
SSD_CONV_CH = 1536
RWKV_WIDTH = 512
RWKV_HEADS = 8
RWKV_HEAD_DIM = 64
RWKV_LORA = 64
RWKV_GATE_LORA = 128
RWKV_LN_EPS = 64e-5
RWKV_COLS = 1792
RET_WIDTH = 512
RET_HEADS = 4
RET_QK = 64
RET_V = 128
RET_COLS = 1536
ROPE_BASE = 10000.0
N_EXPERTS = 32
TOP_K = 4
D_EXPERT = 512
SWIGLU_ALPHA = 1.702
SWIGLU_LIMIT = 7.0

LANES = 128
OFF_RWKV = 0
OFF_DT = 1792
OFF_Z = 2048
OFF_XBC = 3072
OFF_RET = 4608
N_PROJ = 6144

SSD_CHUNK = 128
RET_CHUNK = 128
RWKV_CHUNK = 64
SUBLANES = 8
VMEM_LIMIT = 56 * 1024 * 1024

NN = (((1,), (0,)), ((), ()))
NT = (((1,), (1,)), ((), ()))
TN = (((0,), (0,)), ((), ()))


def _dot(a, b, dims=NN):
    return lax.dot_general(a, b, dims, preferred_element_type=F32)


def _bdot(a, b, dims=NN):
    return _dot(a.astype(BF16), b.astype(BF16), dims)


def _split2(x):
    hi = x.astype(BF16)
    lo = (x - hi.astype(F32)).astype(BF16)
    return hi, lo


def _split3(x):
    hi = x.astype(BF16)
    r1 = x - hi.astype(F32)
    mid = r1.astype(BF16)
    lo = (r1 - mid.astype(F32)).astype(BF16)
    return hi, mid, lo


def _dot_onehot_rhs(x, e, parts=3):
    pieces = _split3(x) if parts == 3 else _split2(x)
    out = _dot(pieces[0], e)
    for p in pieces[1:]:
        out = out + _dot(p, e)
    return out


def _dot_onehot_lhs(e, x):
    pieces = _split3(x)
    out = _dot(e, pieces[0])
    for p in pieces[1:]:
        out = out + _dot(e, p)
    return out


def _dot_hi(a, b):
    ah, al = _split2(a)
    bh, bl = _split2(b)
    return _dot(ah, bh) + _dot(al, bh) + _dot(ah, bl)


def _sigmoid(x):
    return 1.0 / (1.0 + jnp.exp(-x))


def _silu(x):
    return x * _sigmoid(x)


def _softplus(x):
    return jnp.maximum(x, 0.0) + jnp.log1p(jnp.exp(-jnp.abs(x)))


def _iota(shape, dim):
    return lax.broadcasted_iota(jnp.int32, shape, dim)


def _params(sem):
    return pltpu.CompilerParams(dimension_semantics=sem, vmem_limit_bytes=VMEM_LIMIT)


def _mod_kernel(cb_ref, w_ref, b_ref, o_ref):
    nb, d, _ = cb_ref.shape
    tn = w_ref.shape[2]
    kc = 32
    for b in range(nb):
        def body(i, acc, b=b):
            rows = pl.ds(pl.multiple_of(i * kc, kc), kc)
            cond = _silu(cb_ref[b, rows, :])
            w = w_ref[0, rows, :]
            parts = []
            for j in range(tn // LANES):
                p = w[:, j * LANES:(j + 1) * LANES] * cond
                parts.append(p[0:8] + p[8:16] + p[16:24] + p[24:32])
            return acc + jnp.concatenate(parts, axis=1)
        acc = lax.fori_loop(0, d // kc, body, jnp.zeros((SUBLANES, tn), F32))
        o_ref[0, b:b + 1, :] = jnp.sum(acc, axis=0, keepdims=True) + b_ref[0]


def _modulation(c, ada_w, ada_b):
    depth, d, n6 = ada_w.shape
    nb = c.shape[0]
    tn = 1024
    cb = jnp.broadcast_to(c[:, :, None], (nb, d, LANES))
    return pl.pallas_call(
        _mod_kernel,
        out_shape=jax.ShapeDtypeStruct((depth, nb, n6), F32),
        grid=(depth, n6 // tn),
        in_specs=[pl.BlockSpec((nb, d, LANES), lambda l, j: (0, 0, 0)),
                  pl.BlockSpec((1, d, tn), lambda l, j: (l, 0, j)),
                  pl.BlockSpec((1, 1, tn), lambda l, j: (l, 0, j))],
        out_specs=pl.BlockSpec((1, nb, tn), lambda l, j: (l, 0, j)),
        compiler_params=_params(("arbitrary", "arbitrary")),
        name="adaln_mod",
    )(cb, ada_w, ada_b.reshape(depth, 1, n6))


def _rope_kernel(pos_ref, cos_ref, sin_ref):
    width = cos_ref.shape[1]
    pos = pos_ref[...].astype(F32)
    lane = _iota((1, width), 1)
    half = RET_QK // 2
    inv_freq = jnp.exp((lane & (half - 1)).astype(F32) * (-math.log(ROPE_BASE) / half))
    ang = pos * inv_freq
    sign = jnp.where((lane & (RET_QK - 1)) < half, -1.0, 1.0)
    cos_ref[...] = jnp.cos(ang)
    sin_ref[...] = jnp.sin(ang) * sign


def _rope_tables(positions):
    n = positions.size
    tm = min(n, 1024)
    width = RET_HEADS * RET_QK
    return pl.pallas_call(
        _rope_kernel,
        out_shape=(jax.ShapeDtypeStruct((n, width), F32), jax.ShapeDtypeStruct((n, width), F32)),
        grid=(n // tm,),
        in_specs=[pl.BlockSpec((tm, 1), lambda i: (i, 0))],
        out_specs=(pl.BlockSpec((tm, width), lambda i: (i, 0)), pl.BlockSpec((tm, width), lambda i: (i, 0))),
        compiler_params=_params(("arbitrary",)),
        name="rope_tables",
    )(positions.reshape(n, 1))


def _rms_mod(x, nw, scale, shift):
    ms = jnp.mean(x * x, axis=-1, keepdims=True)
    return x * lax.rsqrt(ms + RMS_EPS) * nw * (1.0 + scale) + shift


def _inproj_kernel(x_ref, mod_ref, nw_ref, w_ref, o_ref, h_ref):
    @pl.when(pl.program_id(1) == 0)
    def _():
        tm = x_ref.shape[0]
        rb = 256
        for r in range(tm // rb):
            rows = slice(r * rb, (r + 1) * rb)
            h = _rms_mod(x_ref[rows, :], nw_ref[...], mod_ref[0, 1:2, :], mod_ref[0, 0:1, :])
            h_ref[rows, :] = h.astype(BF16)
    o_ref[...] = _dot(h_ref[...], w_ref[...])


def _inproj(x2, mod_l, nw, w_p, seq):
    n, d = x2.shape
    npj = w_p.shape[1]
    tm = min(1024, seq)
    tn = 1024
    per_batch = seq // tm
    return pl.pallas_call(
        _inproj_kernel,
        out_shape=jax.ShapeDtypeStruct((n, npj), F32),
        grid=(n // tm, npj // tn),
        in_specs=[pl.BlockSpec((tm, d), lambda i, j: (i, 0)),
                  pl.BlockSpec((1, 6, d), lambda i, j: (i // per_batch, 0, 0)),
                  pl.BlockSpec((1, d), lambda i, j: (0, 0)),
                  pl.BlockSpec((d, tn), lambda i, j: (0, j))],
        out_specs=pl.BlockSpec((tm, tn), lambda i, j: (i, j)),
        scratch_shapes=[pltpu.VMEM((tm, d), BF16)],
        compiler_params=_params(("arbitrary", "arbitrary")),
        name="inproj",
    )(x2, mod_l, nw, w_p)


def _ssd_kernel(z_ref, xbc_ref, dt_ref, cw_ref, cb_ref, dtb_ref, alog_ref, dexp_ref, nw_ref,
                y_ref, buf_ref, st_ref):
    lc = xbc_ref.shape[0]
    c = pl.program_id(1)
    hp = SSD_HEAD_DIM
    gw = SSD_WIDTH // SSD_GROUPS

    @pl.when(c == 0)
    def _():
        buf_ref[...] = jnp.zeros_like(buf_ref)
        st_ref[...] = jnp.zeros_like(st_ref)

    @pl.when(c > 0)
    def _():
        buf_ref[0:SUBLANES, :] = buf_ref[lc:lc + SUBLANES, :]

    buf_ref[SUBLANES:SUBLANES + lc, :] = xbc_ref[...]
    conv = cb_ref[...] + cw_ref[SSD_CONV - 1:SSD_CONV, :] * buf_ref[SUBLANES:SUBLANES + lc, :]
    for j in range(SSD_CONV - 1):
        off = SUBLANES - (SSD_CONV - 1) + j
        conv = conv + cw_ref[j:j + 1, :] * buf_ref[off:off + lc, :]
    u = _silu(conv)
    xs = u[:, :SSD_WIDTH]
    bm = u[:, SSD_WIDTH:SSD_WIDTH + SSD_GROUPS * SSD_STATE].astype(BF16)
    cm = u[:, SSD_WIDTH + SSD_GROUPS * SSD_STATE:].astype(BF16)

    dt = _softplus(dt_ref[...] + dtb_ref[...])
    da = dt * (-jnp.exp(alog_ref[...]))

    row = _iota((lc, lc), 0)
    col = _iota((lc, lc), 1)
    causal = row >= col
    tril = jnp.where(causal, 1.0, 0.0).astype(BF16)
    acum = _dot_onehot_lhs(tril, da)
    acum_t = acum.T

    eh = _iota((LANES, SSD_WIDTH), 0)
    ec = _iota((LANES, SSD_WIDTH), 1)
    expand = jnp.where((ec // hp) == eh, 1.0, 0.0).astype(BF16)
    ex = _dot_onehot_rhs(jnp.concatenate([dt, acum], axis=0), expand)
    dt_x = ex[:lc]
    ac_x = ex[lc:]
    eh2 = _iota((LANES, SSD_HEADS * lc), 0)
    ec2 = _iota((LANES, SSD_HEADS * lc), 1)
    expand2 = jnp.where((ec2 // lc) == eh2, 1.0, 0.0).astype(BF16)
    ac_col = _dot_onehot_rhs(acum, expand2)

    a_last = ac_x[lc - 1:lc, :]
    e_ac = jnp.exp(ac_x)
    e_last = jnp.exp(a_last)
    xdt = xs * dt_x
    xd = (xdt * jnp.exp(a_last - ac_x)).astype(BF16)

    lane = _iota((lc, LANES), 1)
    first = lane < hp
    st = st_ref[...]
    y_parts = []
    for g in range(SSD_GROUPS):
        gs = slice(g * gw, (g + 1) * gw)
        cg = cm[:, g * SSD_STATE:(g + 1) * SSD_STATE]
        bg = bm[:, g * SSD_STATE:(g + 1) * SSD_STATE]
        gmat = _dot(cg, bg, NT)
        y_off = _dot(cg, st[:, gs].astype(BF16)) * e_ac[:, gs]
        for pp in range(gw // LANES):
            h0 = g * (SSD_HEADS // SSD_GROUPS) + 2 * pp
            scs = []
            for q in range(2):
                h = h0 + q
                diff = ac_col[:, h * lc:(h + 1) * lc] - acum_t[h:h + 1, :]
                decay = jnp.where(causal, jnp.exp(jnp.minimum(diff, 0.0)), 0.0)
                scs.append((gmat * decay).astype(BF16))
            xp = xdt[:, h0 * hp:h0 * hp + LANES]
            rhs = jnp.concatenate([jnp.where(first, xp, 0.0), jnp.where(first, 0.0, xp)], axis=0)
            y_diag = _dot(jnp.concatenate(scs, axis=1), rhs.astype(BF16))
            y_parts.append(y_diag + y_off[:, pp * LANES:(pp + 1) * LANES])
        st_ref[:, gs] = st[:, gs] * e_last[:, gs] + _dot(bg, xd[:, gs], TN)
    y = jnp.concatenate(y_parts, axis=1) + dexp_ref[...] * xs
    y = y * _silu(z_ref[...])
    outs = []
    for g in range(SSD_GROUPS):
        yg = y[:, g * gw:(g + 1) * gw]
        outs.append(yg * lax.rsqrt(jnp.mean(yg * yg, axis=-1, keepdims=True) + RMS_EPS))
    y_ref[...] = (jnp.concatenate(outs, axis=1) * nw_ref[...]).astype(BF16)


def _ssd(proj, cw, cb, dtb, alog, dexp, nw, nb, seq):
    n = proj.shape[0]
    lc = SSD_CHUNK
    nc = seq // lc
    row = lambda b, c: b * nc + c
    const = lambda b, c: (0, 0)
    return pl.pallas_call(
        _ssd_kernel,
        out_shape=jax.ShapeDtypeStruct((n, SSD_WIDTH), BF16),
        grid=(nb, nc),
        in_specs=[pl.BlockSpec((lc, SSD_WIDTH), lambda b, c: (row(b, c), OFF_Z // SSD_WIDTH)),
                  pl.BlockSpec((lc, SSD_CONV_CH), lambda b, c: (row(b, c), OFF_XBC // SSD_CONV_CH)),
                  pl.BlockSpec((lc, LANES), lambda b, c: (row(b, c), OFF_DT // LANES)),
                  pl.BlockSpec((SSD_CONV, SSD_CONV_CH), const),
                  pl.BlockSpec((1, SSD_CONV_CH), const),
                  pl.BlockSpec((1, LANES), const),
                  pl.BlockSpec((1, LANES), const),
                  pl.BlockSpec((1, SSD_WIDTH), const),
                  pl.BlockSpec((1, SSD_WIDTH), const)],
        out_specs=pl.BlockSpec((lc, SSD_WIDTH), lambda b, c: (row(b, c), 0)),
        scratch_shapes=[pltpu.VMEM((lc + SUBLANES, SSD_CONV_CH), F32),
                        pltpu.VMEM((SSD_STATE, SSD_WIDTH), F32)],
        compiler_params=_params(("arbitrary", "arbitrary")),
        name="ssd_mixer",
    )(proj, proj, proj, cw, cb, dtb, alog, dexp, nw)


def _ret_kernel(p_ref, cos_ref, sin_ref, y_ref, s_ref):
    lc = p_ref.shape[0]
    c = pl.program_id(1)
    qw = RET_HEADS * RET_QK

    @pl.when(c == 0)
    def _():
        s_ref[...] = jnp.zeros_like(s_ref)

    q = p_ref[:, 0:qw]
    k = p_ref[:, qw:2 * qw]
    gate = p_ref[:, 2 * qw + RET_WIDTH:]
    cosv = cos_ref[...]
    sinv = sin_ref[...]
    half = RET_QK // 2
    lane_q = _iota((lc, qw), 1)
    lower = (lane_q & (RET_QK - 1)) < half

    def rope(t):
        swapped = jnp.where(lower, pltpu.roll(t, qw - half, axis=1), pltpu.roll(t, half, axis=1))
        return t * cosv + swapped * sinv

    q = rope(q)
    k = rope(k) * (RET_QK ** -0.5)

    ri = _iota((lc, lc), 0)
    ci = _iota((lc, lc), 1)
    dist = (ri - ci).astype(F32)
    rowf = _iota((lc, LANES), 0).astype(F32)
    lane = _iota((lc, LANES), 1)
    srow = _iota((LANES, LANES), 0)
    outs = []
    for pr in range(RET_HEADS // 2):
        qp = q[:, pr * LANES:(pr + 1) * LANES]
        kp = k[:, pr * LANES:(pr + 1) * LANES]
        kpb = kp.astype(BF16)
        sp = s_ref[pr]
        spb = sp.astype(BF16)
        lg = [math.log1p(-2.0 ** (-5 - (2 * pr + hh))) for hh in range(2)]
        new = sp * jnp.where(srow < RET_QK, math.exp(lc * lg[0]), math.exp(lc * lg[1]))
        for hh in range(2):
            h = 2 * pr + hh
            mine = (lane < RET_QK) if hh == 0 else (lane >= RET_QK)
            qm = jnp.where(mine, qp, 0.0).astype(BF16)
            vh = p_ref[:, 2 * qw + h * RET_V:2 * qw + (h + 1) * RET_V].astype(BF16)
            inner = jnp.where(dist >= 0.0, jnp.exp(jnp.maximum(dist, 0.0) * lg[hh]), 0.0)
            scores = _dot(qm, kpb, NT) * inner
            o = _dot(scores.astype(BF16), vh) + _dot(qm, spb) * jnp.exp((rowf + 1.0) * lg[hh])
            km = jnp.where(mine, kp, 0.0) * jnp.exp((lc - 1.0 - rowf) * lg[hh])
            new = new + _dot(km.astype(BF16), vh, TN)
            outs.append(o * lax.rsqrt(jnp.mean(o * o, axis=-1, keepdims=True) + RMS_EPS))
        s_ref[pr] = new
    y_ref[...] = (jnp.concatenate(outs, axis=1) * _silu(gate)).astype(BF16)


def _retention(proj, cos_t, sin_t, nb, seq):
    n = proj.shape[0]
    lc = RET_CHUNK
    nc = seq // lc
    qw = RET_HEADS * RET_QK
    return pl.pallas_call(
        _ret_kernel,
        out_shape=jax.ShapeDtypeStruct((n, RET_WIDTH), BF16),
        grid=(nb, nc),
        in_specs=[pl.BlockSpec((lc, RET_COLS), lambda b, c: (b * nc + c, OFF_RET // RET_COLS)),
                  pl.BlockSpec((lc, qw), lambda b, c: (b * nc + c, 0)),
                  pl.BlockSpec((lc, qw), lambda b, c: (b * nc + c, 0))],
        out_specs=pl.BlockSpec((lc, RET_WIDTH), lambda b, c: (b * nc + c, 0)),
        scratch_shapes=[pltpu.VMEM((RET_HEADS // 2, LANES, RET_V), F32)],
        compiler_params=_params(("arbitrary", "arbitrary")),
        name="retention_mixer",
    )(proj, cos_t, sin_t)


def _rwkv_kernel(p_ref, mu_ref, w0_ref, wup_ref, a0_ref, aup_ref, gup_ref, kk_ref, ka_ref, rk_ref,
                 lnw_ref, lnb_ref, hs_ref, y_ref, buf_ref, z_ref):
    lc = p_ref.shape[0]
    c = pl.program_id(1)
    w = RWKV_WIDTH
    hd = RWKV_HEAD_DIM

    @pl.when(c == 0)
    def _():
        buf_ref[...] = jnp.zeros_like(buf_ref)
        z_ref[...] = jnp.zeros_like(z_ref)

    @pl.when(c > 0)
    def _():
        buf_ref[0:SUBLANES, :] = buf_ref[lc:lc + SUBLANES, :]

    p = p_ref[...]
    buf_ref[SUBLANES:SUBLANES + lc, :] = p
    prev = buf_ref[SUBLANES - 1:SUBLANES - 1 + lc, :]
    pf = p + (prev - p) * mu_ref[...]
    r = pf[:, 0:w]
    k = pf[:, w:2 * w]
    v = pf[:, 2 * w:3 * w]
    lora_in = pf[:, 3 * w:3 * w + LANES]
    g_in = pf[:, 3 * w + LANES:]

    hs = hs_ref[...]

    def headsum(x):
        return _dot_onehot_rhs(x, hs, parts=2)

    wraw = -_softplus(-(w0_ref[...] + _dot_hi(jnp.tanh(lora_in), wup_ref[...]))) - 0.5
    lw = -jnp.exp(wraw)
    a_sig = _sigmoid(a0_ref[...] + _dot_hi(lora_in, aup_ref[...]))
    gate = _dot_hi(_sigmoid(g_in), gup_ref[...])
    kk = k * kk_ref[...]
    kk = kk * lax.rsqrt(jnp.maximum(headsum(kk * kk), 1e-24))
    k2 = k * (1.0 + (a_sig - 1.0) * ka_ref[...])
    av = -kk
    bv = kk * a_sig

    tr = _iota((lc, lc), 0)
    tc = _iota((lc, lc), 1)
    tril = jnp.where(tr >= tc, 1.0, 0.0).astype(BF16)
    cum = _dot_onehot_lhs(tril, lw)
    cum_last = cum[lc - 1:lc, :]
    e_pos = jnp.exp(cum)
    e_neg = jnp.exp(-cum)
    e_tail = jnp.exp(cum_last - cum)
    g_last = jnp.exp(cum_last)
    at = av * jnp.exp(cum - lw)
    rt = r * e_pos
    bt = bv * e_neg
    kt = k2 * e_neg
    bh = bv * e_tail
    kh = k2 * e_tail

    lane = _iota((lc, LANES), 1)
    first = lane < hd
    r2 = _iota((LANES, LANES), 0)
    c2 = _iota((LANES, LANES), 1)
    ti = r2 & (lc - 1)
    sj = c2 & (lc - 1)
    bottom = jnp.where(r2 < lc, 0, 1)
    mask0 = (ti + bottom) > sj
    mask1 = (ti + 1 - bottom) > sj
    tl = jnp.maximum(r2, c2) < lc
    br = jnp.minimum(r2, c2) >= lc
    bdiag = (r2 // lc) == (c2 // lc)
    eye = jnp.where(r2 == c2, 1.0, 0.0)
    zeros = jnp.zeros((lc, LANES), F32)

    ys = []
    for j in range(w // LANES):
        sl = slice(j * LANES, (j + 1) * LANES)
        at_p, rt_p, bt_p, kt_p, bh_p, kh_p, v_p = at[:, sl], rt[:, sl], bt[:, sl], kt[:, sl], bh[:, sl], kh[:, sl], v[:, sl]
        at0 = jnp.where(first, at_p, 0.0)
        at1 = jnp.where(first, 0.0, at_p)
        lhs0 = jnp.concatenate([at0, jnp.where(first, rt_p, 0.0)], axis=0)
        rhs0 = jnp.concatenate([bt_p, kt_p], axis=0)
        a0m = jnp.where(mask0, _bdot(lhs0, rhs0, NT), 0.0)
        lhs1 = jnp.concatenate([jnp.where(first, 0.0, rt_p), at1], axis=0)
        rhs1 = jnp.concatenate([kt_p, bt_p], axis=0)
        a1m = jnp.where(mask1, _bdot(lhs1, rhs1, NT), 0.0)
        a0b = a0m.astype(BF16)
        a1b = a1m.astype(BF16)

        nil = jnp.where(tl, a0m, 0.0) + jnp.where(br, a1m, 0.0)
        tm = eye + nil
        pw = nil
        for _ in range(5):
            pb = pw.astype(BF16)
            pw = _dot(pb, pb)
            tm = tm + _bdot(tm, pw)

        v0 = jnp.where(first, v_p, 0.0)
        v1 = jnp.where(first, 0.0, v_p)
        g0 = _dot(a0b, jnp.concatenate([zeros, v0], axis=0).astype(BF16))
        g1 = _dot(a1b, jnp.concatenate([v1, zeros], axis=0).astype(BF16))
        akv = jnp.concatenate([g0[0:lc], g1[lc:]], axis=0)
        arkv = g0[lc:] + g1[0:lc]
        wu = _bdot(tm, jnp.concatenate([jnp.concatenate([at0, at1], axis=0), akv], axis=1))
        wmat = wu[0:lc, 0:LANES] + wu[lc:, 0:LANES]
        u0 = wu[0:lc, LANES:] + wu[lc:, LANES:]

        zb = z_ref[j]
        zbb = zb.astype(BF16)
        u = _dot(wmat.astype(BF16), zbb) + u0
        um0 = jnp.where(first, u, 0.0)
        um1 = jnp.where(first, 0.0, u)
        y = (_dot(rt_p.astype(BF16), zbb) + arkv
             + _dot(a0b[lc:], jnp.concatenate([um0, zeros], axis=0).astype(BF16))
             + _dot(a1b[0:lc], jnp.concatenate([zeros, um1], axis=0).astype(BF16)))
        ys.append(y)
        upd = _bdot(jnp.concatenate([bh_p, kh_p], axis=0), jnp.concatenate([u, v_p], axis=0), TN)
        g_col = jnp.broadcast_to(g_last[:, sl], (LANES, LANES)).T
        z_ref[j] = g_col * zb + jnp.where(bdiag, upd, 0.0)

    y = jnp.concatenate(ys, axis=1)
    mean = headsum(y) * (1.0 / hd)
    dev = y - mean
    var = headsum(dev * dev) * (1.0 / hd)
    yn = dev * lax.rsqrt(var + RWKV_LN_EPS) * lnw_ref[...] + lnb_ref[...]
    bonus = headsum(r * k2 * rk_ref[...]) * v
    y_ref[...] = ((yn + bonus) * gate).astype(BF16)


def _rwkv(proj, mu, w0, wup, a0, aup, gup, kkp, kap, rk, lnw, lnb, hs, nb, seq):
    n = proj.shape[0]
    lc = RWKV_CHUNK
    nc = seq // lc
    const = lambda b, c: (0, 0)
    vec = pl.BlockSpec((1, RWKV_WIDTH), const)
    lora = pl.BlockSpec((LANES, RWKV_WIDTH), const)
    return pl.pallas_call(
        _rwkv_kernel,
        out_shape=jax.ShapeDtypeStruct((n, RWKV_WIDTH), BF16),
        grid=(nb, nc),
        in_specs=[pl.BlockSpec((lc, RWKV_COLS), lambda b, c: (b * nc + c, OFF_RWKV // RWKV_COLS)),
                  pl.BlockSpec((1, RWKV_COLS), const),
                  vec, lora, vec, lora, lora, vec, vec, vec, vec, vec,
                  pl.BlockSpec((RWKV_WIDTH, RWKV_WIDTH), const)],
        out_specs=pl.BlockSpec((lc, RWKV_WIDTH), lambda b, c: (b * nc + c, 0)),
        scratch_shapes=[pltpu.VMEM((lc + SUBLANES, RWKV_COLS), F32),
                        pltpu.VMEM((RWKV_WIDTH // LANES, LANES, LANES), F32)],
        compiler_params=_params(("arbitrary", "arbitrary")),
        name="rwkv7_mixer",
    )(proj, mu, w0, wup, a0, aup, gup, kkp, kap, rk, lnw, lnb, hs)


def _outproj_kernel(ys_ref, yr_ref, yt_ref, w_ref, x_ref, mod_ref, nw_ref, rw_ref, rb_ref,
                    xo_ref, h_ref, cmb_ref):
    y = jnp.concatenate([ys_ref[...], yr_ref[...], yt_ref[...]], axis=1)
    xn = x_ref[...] + mod_ref[0, 2:3, :] * _dot(y, w_ref[...])
    xo_ref[...] = xn
    h = _rms_mod(xn, nw_ref[...], mod_ref[0, 4:5, :], mod_ref[0, 3:4, :])
    h_ref[...] = h.astype(BF16)
    logits = _dot_hi(h, rw_ref[...]) + rb_ref[...]
    lane = _iota(logits.shape, 1).astype(F32)
    work = logits
    comb = jnp.zeros_like(logits)
    top1 = None
    for i in range(TOP_K):
        mx = jnp.max(work, axis=-1, keepdims=True)
        idx = jnp.min(jnp.where(work == mx, lane, float(LANES)), axis=-1, keepdims=True)
        hit = lane == idx
        if i == 0:
            top1 = mx
        comb = comb + jnp.where(hit, jnp.exp(mx - top1), 0.0)
        work = jnp.where(hit, -jnp.inf, work)
    cmb_ref[...] = comb / jnp.sum(comb, axis=-1, keepdims=True)


def _outproj(y_ssd, y_rwkv, y_ret, w_out, x2, mod_l, nw, rw, rb, seq):
    n, d = x2.shape
    tm = min(512, seq)
    per_batch = seq // tm
    row = lambda i: (i, 0)
    const = lambda i: (0, 0)
    return pl.pallas_call(
        _outproj_kernel,
        out_shape=(jax.ShapeDtypeStruct((n, d), F32), jax.ShapeDtypeStruct((n, d), BF16),
                   jax.ShapeDtypeStruct((n, LANES), F32)),
        grid=(n // tm,),
        in_specs=[pl.BlockSpec((tm, SSD_WIDTH), row), pl.BlockSpec((tm, RWKV_WIDTH), row),
                  pl.BlockSpec((tm, RET_WIDTH), row), pl.BlockSpec((d, d), const),
                  pl.BlockSpec((tm, d), row),
                  pl.BlockSpec((1, 6, d), lambda i: (i // per_batch, 0, 0)),
                  pl.BlockSpec((1, d), const), pl.BlockSpec((d, LANES), const),
                  pl.BlockSpec((1, LANES), const)],
        out_specs=(pl.BlockSpec((tm, d), row), pl.BlockSpec((tm, d), row), pl.BlockSpec((tm, LANES), row)),
        compiler_params=_params(("arbitrary",)),
        name="outproj_router",
    )(y_ssd, y_rwkv, y_ret, w_out, x2, mod_l, nw, rw, rb)


def _moe_kernel(h_ref, cmb_ref, wg_ref, wu_ref, bg_ref, bu_ref, wd_ref, bd_ref, x_ref, mod_ref,
                o_ref, acc_ref):
    e = pl.program_id(1)

    @pl.when(e == 0)
    def _():
        acc_ref[...] = jnp.zeros_like(acc_ref)

    h = h_ref[...]
    gate = jnp.minimum(_dot(h, wg_ref[0]) + bg_ref[0], SWIGLU_LIMIT)
    up = jnp.clip(_dot(h, wu_ref[0]) + bu_ref[0], -SWIGLU_LIMIT, SWIGLU_LIMIT)
    act = (up + 1.0) * gate * _sigmoid(gate * SWIGLU_ALPHA)
    out = _dot(act.astype(BF16), wd_ref[0]) + bd_ref[0]
    cmb = cmb_ref[...]
    lane = _iota(cmb.shape, 1)
    ce = jnp.sum(jnp.where(lane == e, cmb, 0.0), axis=-1, keepdims=True)
    acc_ref[...] += ce * out

    @pl.when(e == pl.num_programs(1) - 1)
    def _():
        o_ref[...] = x_ref[...] + mod_ref[0, 5:6, :] * acc_ref[...]


def _moe(h2, cmb, wg, wu, bg, bu, wd, bd, x2, mod_l, seq):
    n, d = x2.shape
    ne, _, de = wg.shape
    tm = min(512, seq)
    per_batch = seq // tm
    row = lambda i, e: (i, 0)
    exp3 = lambda i, e: (e, 0, 0)
    return pl.pallas_call(
        _moe_kernel,
        out_shape=jax.ShapeDtypeStruct((n, d), F32),
        grid=(n // tm, ne),
        in_specs=[pl.BlockSpec((tm, d), row), pl.BlockSpec((tm, LANES), row),
                  pl.BlockSpec((1, d, de), exp3), pl.BlockSpec((1, d, de), exp3),
                  pl.BlockSpec((1, 1, de), exp3), pl.BlockSpec((1, 1, de), exp3),
                  pl.BlockSpec((1, de, d), exp3), pl.BlockSpec((1, 1, d), exp3),
                  pl.BlockSpec((tm, d), row),
                  pl.BlockSpec((1, 6, d), lambda i, e: (i // per_batch, 0, 0))],
        out_specs=pl.BlockSpec((tm, d), row),
        scratch_shapes=[pltpu.VMEM((tm, d), F32)],
        compiler_params=_params(("arbitrary", "arbitrary")),
        name="moe_experts",
    )(h2, cmb, wg, wu, bg, bu, wd, bd, x2, mod_l)


def _final_kernel(x_ref, nw_ref, o_ref):
    x = x_ref[...]
    ms = jnp.mean(x * x, axis=-1, keepdims=True)
    o_ref[...] = x * lax.rsqrt(ms + RMS_EPS) * nw_ref[...]


def _final_norm(x2, nw):
    n, d = x2.shape
    tm = min(512, n)
    return pl.pallas_call(
        _final_kernel,
        out_shape=jax.ShapeDtypeStruct((n, d), F32),
        grid=(n // tm,),
        in_specs=[pl.BlockSpec((tm, d), lambda i: (i, 0)), pl.BlockSpec((1, d), lambda i: (0, 0))],
        out_specs=pl.BlockSpec((tm, d), lambda i: (i, 0)),
        compiler_params=_params(("arbitrary",)),
        name="final_norm",
    )(x2, nw)


def _rwkv_cols(a):
    w = RWKV_WIDTH
    lo = RWKV_LORA
    r, wl, k, v, al, gl = (a[..., 0:w], a[..., w:w + lo], a[..., w + lo:2 * w + lo],
                           a[..., 2 * w + lo:3 * w + lo], a[..., 3 * w + lo:3 * w + 2 * lo], a[..., 3 * w + 2 * lo:])
    return jnp.concatenate([r, k, v, wl, al, gl], axis=-1)


def _prep_w_in(w):
    d = w.shape[0]
    ssd_cols = SSD_WIDTH + SSD_CONV_CH + SSD_HEADS
    z = w[:, 0:SSD_WIDTH]
    xbc = w[:, SSD_WIDTH:SSD_WIDTH + SSD_CONV_CH]
    dt = w[:, SSD_WIDTH + SSD_CONV_CH:ssd_cols]
    rw = _rwkv_cols(w[:, ssd_cols:ssd_cols + RWKV_COLS])
    ret = w[:, ssd_cols + RWKV_COLS:]
    pad = jnp.zeros((d, OFF_Z - OFF_DT - SSD_HEADS), w.dtype)
    return jnp.concatenate([rw, dt, pad, z, xbc, ret], axis=1).astype(BF16)


def _pad_lanes(a, value=0.0):
    return jnp.pad(a, ((0, 0), (0, LANES - a.shape[1])), constant_values=value)


def kernel(x, c, positions, ada_w, ada_b, norm_mix, norm_ffn, norm_final, w_in, w_out, ssd_conv_w, ssd_conv_b, ssd_dt_bias, ssd_a_log, ssd_d, ssd_norm, rwkv_mu, rwkv_w0, rwkv_w_up, rwkv_a0, rwkv_a_up, rwkv_g_up, rwkv_k_k, rwkv_k_a, rwkv_r_k, rwkv_ln_w, rwkv_ln_b, router_w, router_b, moe_w_gate_up, moe_b_gate_up, moe_w_down, moe_b_down):
    nb, seq, d = x.shape
    depth = ada_w.shape[0]
    n = nb * seq
    x2 = x.reshape(n, d)

    mod = _modulation(c, ada_w, ada_b).reshape(depth, nb, 6, d)
    cos_t, sin_t = _rope_tables(positions)
    hi = _iota((RWKV_WIDTH, RWKV_WIDTH), 0) // RWKV_HEAD_DIM
    hj = _iota((RWKV_WIDTH, RWKV_WIDTH), 1) // RWKV_HEAD_DIM
    head_same = (hi == hj).astype(BF16)
    zlora = jnp.zeros((RWKV_LORA, RWKV_WIDTH), F32)

    for l in range(depth):
        row = lambda a: a[l][None, :]
        proj = _inproj(x2, mod[l], row(norm_mix), _prep_w_in(w_in[l]), seq)
        y_ssd = _ssd(proj, ssd_conv_w[l].T, row(ssd_conv_b), _pad_lanes(row(ssd_dt_bias)),
                     _pad_lanes(row(ssd_a_log)), jnp.repeat(ssd_d[l], SSD_HEAD_DIM)[None, :],
                     row(ssd_norm), nb, seq)
        y_rwkv = _rwkv(proj, _rwkv_cols(row(rwkv_mu)), row(rwkv_w0),
                       jnp.concatenate([rwkv_w_up[l], zlora], axis=0), row(rwkv_a0),
                       jnp.concatenate([zlora, rwkv_a_up[l]], axis=0), rwkv_g_up[l],
                       row(rwkv_k_k), row(rwkv_k_a), rwkv_r_k[l].reshape(1, RWKV_WIDTH),
                       row(rwkv_ln_w), row(rwkv_ln_b), head_same, nb, seq)
        y_ret = _retention(proj, cos_t, sin_t, nb, seq)
        x2, h2, cmb = _outproj(y_ssd, y_rwkv, y_ret, w_out[l].astype(BF16), x2, mod[l], row(norm_ffn),
                               _pad_lanes(router_w[l]), _pad_lanes(row(router_b), -1e30), seq)
        wgu = moe_w_gate_up[l]
        bgu = moe_b_gate_up[l]
        x2 = _moe(h2, cmb, wgu[:, :, 0::2].astype(BF16), wgu[:, :, 1::2].astype(BF16),
                  bgu[:, None, 0::2], bgu[:, None, 1::2], moe_w_down[l].astype(BF16),
                  moe_b_down[l][:, None, :], x2, mod[l], seq)
    return _final_norm(x2, norm_final[None, :]).reshape(nb, seq, d)
```
